```python
import math
import jax, jax.numpy as jnp
from jax import lax
import numpy as np

D_MODEL = 1024
BATCH = 8
SEQ = 2048
DEPTH = 4
DEC_BATCH = 16
DEC_SEQ = 2048
PAST_LEN = 128

MEM_LEN = 256
D_FF = 2816
C_A = D_MODEL // 2
A_GROUPS = 4
A_GROUP_DIM = C_A // A_GROUPS
C_B = D_MODEL - C_A
CONV_WIDTH = 31
CONV_PAD = (CONV_WIDTH - 1) // 2
AB_IN = C_A + 2 * C_B
MLA_HEADS = 8
QK_NOPE = 128
QK_ROPE = 64
V_DIM = 128
Q_LORA = 512
KV_LORA = 256
MLA_IN = Q_LORA + KV_LORA + QK_ROPE
ROPE_BASE = 10000.0
Q_BLOCK = 128
XA_HEADS = 4
XA_HEAD_DIM = D_MODEL // XA_HEADS
N_AB = (DEPTH + 1) // 2
N_MLA = DEPTH // 2
EPS = 1e-6

kernel_name = "hybrid_fnet_conv_mla_encoder"


def rms_norm(x, g):
    xf = x.astype(jnp.float32)
    y = xf * lax.rsqrt(jnp.mean(xf * xf, axis=-1, keepdims=True) + EPS)
    return (y * g.astype(jnp.float32)).astype(x.dtype)


def layer_norm(x, g, b):
    xf = x.astype(jnp.float32)
    mu = jnp.mean(xf, axis=-1, keepdims=True)
    xc = xf - mu
    var = jnp.mean(xc * xc, axis=-1, keepdims=True)
    y = xc * lax.rsqrt(var + EPS) * g.astype(jnp.float32) + b.astype(jnp.float32)
    return y.astype(x.dtype)


def swiglu(h, w_gate, w_up, w_down):
    return (jax.nn.silu(h @ w_gate) * (h @ w_up)) @ w_down


def rope_tables(seq):
    inv_freq = 1.0 / (ROPE_BASE ** (jnp.arange(0, QK_ROPE, 2, dtype=jnp.float32) / QK_ROPE))
    ang = jnp.arange(seq, dtype=jnp.float32)[:, None] * inv_freq[None, :]
    return jnp.cos(ang), jnp.sin(ang)


def apply_rope(x, cos, sin):
    half = x.shape[-1] // 2
    x1 = x[..., :half].astype(jnp.float32)
    x2 = x[..., half:].astype(jnp.float32)
    return jnp.concatenate([x1 * cos - x2 * sin, x1 * sin + x2 * cos], axis=-1).astype(x.dtype)


def fourier_mix(u):
    B, S, _ = u.shape
    uh = u.reshape(B, S, A_GROUPS, A_GROUP_DIM).astype(jnp.float32)
    y = jnp.fft.fft2(uh, axes=(1, 3), norm="ortho").real
    return y.reshape(B, S, C_A).astype(u.dtype)


def conv_module(u, conv_w, conv_b, ln_g, ln_b):
    a, g = u[..., :C_B], u[..., C_B:]
    h = a * jax.nn.sigmoid(g)
    h = lax.conv_general_dilated(
        h, conv_w[:, None, :].astype(h.dtype), window_strides=(1,),
        padding=[(CONV_PAD, CONV_PAD)], dimension_numbers=("NWC", "WIO", "NWC"),
        feature_group_count=C_B) + conv_b
    return jax.nn.silu(layer_norm(h, ln_g, ln_b))


def mixer_ab(h, w_in, conv_w, conv_b, ln_g, ln_b, w_out):
    u = h @ w_in
    ya = fourier_mix(u[..., :C_A])
    yb = conv_module(u[..., C_A:], conv_w, conv_b, ln_g, ln_b)
    return jnp.concatenate([ya, yb], axis=-1) @ w_out


def blocked_attention(q, k, v, scale):
    B, S, H, Dq = q.shape
    nb = S // Q_BLOCK
    qb = q.reshape(B, nb, Q_BLOCK, H, Dq).transpose(1, 0, 2, 3, 4)

    def one_block(qblk):
        s = jnp.einsum("bqhd,bkhd->bhqk", qblk, k).astype(jnp.float32) * scale
        p = jax.nn.softmax(s, axis=-1).astype(v.dtype)
        return jnp.einsum("bhqk,bkhd->bqhd", p, v)

    o = lax.map(one_block, qb)
    return o.transpose(1, 0, 2, 3, 4).reshape(B, S, H, v.shape[-1])


def mla(h, w_in, q_norm, w_q_b, kv_norm, w_kv_b, w_out, cos, sin):
    B, S, _ = h.shape
    u = h @ w_in
    cq = u[..., :Q_LORA]
    ckv = u[..., Q_LORA:Q_LORA + KV_LORA]
    kr = u[..., Q_LORA + KV_LORA:]
    q = (rms_norm(cq, q_norm) @ w_q_b).reshape(B, S, MLA_HEADS, QK_NOPE + QK_ROPE)
    q = jnp.concatenate([q[..., :QK_NOPE],
                         apply_rope(q[..., QK_NOPE:], cos[None, :, None, :], sin[None, :, None, :])], axis=-1)
    kv = (rms_norm(ckv, kv_norm) @ w_kv_b).reshape(B, S, MLA_HEADS, QK_NOPE + V_DIM)
    k_nope, v = kv[..., :QK_NOPE], kv[..., QK_NOPE:]
    kr = apply_rope(kr, cos[None], sin[None])
    k = jnp.concatenate([k_nope, jnp.broadcast_to(kr[:, :, None, :], (B, S, MLA_HEADS, QK_ROPE))], axis=-1)
    o = blocked_attention(q, k, v, 1.0 / math.sqrt(QK_NOPE + QK_ROPE))
    return o.reshape(B, S, MLA_HEADS * V_DIM) @ w_out


def cross_attn(h, m, w_q, w_kv, w_o):
    B, S, _ = h.shape
    M = m.shape[1]
    q = (h @ w_q).reshape(B, S, XA_HEADS, XA_HEAD_DIM)
    kv = (m @ w_kv).reshape(B, M, 2, XA_HEADS, XA_HEAD_DIM)
    k, v = kv[:, :, 0], kv[:, :, 1]
    s = jnp.einsum("bshd,bmhd->bhsm", q, k).astype(jnp.float32) * (1.0 / math.sqrt(XA_HEAD_DIM))
    p = jax.nn.softmax(s, axis=-1).astype(v.dtype)
    o = jnp.einsum("bhsm,bmhd->bshd", p, v).reshape(B, S, D_MODEL)
    return o @ w_o


def trunk(x, mem, p):
    cos, sin = rope_tables(x.shape[1])
    for l in range(DEPTH):
        h = rms_norm(x, p["ffn1_norm"][l])
        x = x + 0.5 * swiglu(h, p["ffn1_w_gate"][l], p["ffn1_w_up"][l], p["ffn1_w_down"][l])
        h = rms_norm(x, p["mix_norm"][l])
        i = l // 2
        if l % 2 == 0:
            x = x + mixer_ab(h, p["ab_w_in"][i], p["ab_conv_w"][i], p["ab_conv_b"][i],
                             p["ab_conv_ln_g"][i], p["ab_conv_ln_b"][i], p["ab_w_out"][i])
        else:
            x = x + mla(h, p["mla_w_in"][i], p["mla_q_norm"][i], p["mla_w_q_b"][i],
                        p["mla_kv_norm"][i], p["mla_w_kv_b"][i], p["mla_w_out"][i], cos, sin)
        h = rms_norm(x, p["xattn_norm"][l])
        m = rms_norm(mem, p["mem_norm"][l])
        x = x + cross_attn(h, m, p["xattn_w_q"][l], p["xattn_w_kv"][l], p["xattn_w_o"][l])
        h = rms_norm(x, p["ffn2_norm"][l])
        x = x + 0.5 * swiglu(h, p["ffn2_w_gate"][l], p["ffn2_w_up"][l], p["ffn2_w_down"][l])
    return rms_norm(x, p["final_norm"])


def _w(k, shape, fan_in):
    return jax.random.normal(k, shape, jnp.float32) * (fan_in ** -0.5)


def _gain(k, shape):
    return 1.0 + 0.01 * jax.random.normal(k, shape, jnp.float32)


def _small(k, shape):
    return 0.01 * jax.random.normal(k, shape, jnp.float32)


def setup_inputs(seed: int = 0) -> dict:
    key = jax.random.key(seed)
    ks = jax.random.split(key, 32)
    f32 = jnp.float32
    return {
        "x_prompt": jax.random.normal(ks[0], (BATCH, SEQ, D_MODEL), f32),
        "x_sample": jax.random.normal(ks[1], (DEC_BATCH, DEC_SEQ, D_MODEL), f32),
        "mem_prompt": jax.random.normal(ks[2], (BATCH, MEM_LEN, D_MODEL), f32),
        "mem_sample": jax.random.normal(ks[3], (DEC_BATCH, MEM_LEN, D_MODEL), f32),
        "ffn1_norm": _gain(ks[4], (DEPTH, D_MODEL)),
        "ffn1_w_gate": _w(ks[5], (DEPTH, D_MODEL, D_FF), D_MODEL),
        "ffn1_w_up": _w(ks[6], (DEPTH, D_MODEL, D_FF), D_MODEL),
        "ffn1_w_down": _w(ks[7], (DEPTH, D_FF, D_MODEL), D_FF),
        "mix_norm": _gain(ks[8], (DEPTH, D_MODEL)),
        "xattn_norm": _gain(ks[9], (DEPTH, D_MODEL)),
        "mem_norm": _gain(ks[10], (DEPTH, D_MODEL)),
        "xattn_w_q": _w(ks[11], (DEPTH, D_MODEL, D_MODEL), D_MODEL),
        "xattn_w_kv": _w(ks[12], (DEPTH, D_MODEL, 2 * D_MODEL), D_MODEL),
        "xattn_w_o": _w(ks[13], (DEPTH, D_MODEL, D_MODEL), D_MODEL),
        "ffn2_norm": _gain(ks[14], (DEPTH, D_MODEL)),
        "ffn2_w_gate": _w(ks[15], (DEPTH, D_MODEL, D_FF), D_MODEL),
        "ffn2_w_up": _w(ks[16], (DEPTH, D_MODEL, D_FF), D_MODEL),
        "ffn2_w_down": _w(ks[17], (DEPTH, D_FF, D_MODEL), D_FF),
        "ab_w_in": _w(ks[18], (N_AB, D_MODEL, AB_IN), D_MODEL),
        "ab_conv_w": _w(ks[19], (N_AB, CONV_WIDTH, C_B), CONV_WIDTH),
        "ab_conv_b": _small(ks[20], (N_AB, C_B)),
        "ab_conv_ln_g": _gain(ks[21], (N_AB, C_B)),
        "ab_conv_ln_b": _small(ks[22], (N_AB, C_B)),
        "ab_w_out": _w(ks[23], (N_AB, C_A + C_B, D_MODEL), C_A + C_B),
        "mla_w_in": _w(ks[24], (N_MLA, D_MODEL, MLA_IN), D_MODEL),
        "mla_q_norm": _gain(ks[25], (N_MLA, Q_LORA)),
        "mla_w_q_b": _w(ks[26], (N_MLA, Q_LORA, MLA_HEADS * (QK_NOPE + QK_ROPE)), Q_LORA),
        "mla_kv_norm": _gain(ks[27], (N_MLA, KV_LORA)),
        "mla_w_kv_b": _w(ks[28], (N_MLA, KV_LORA, MLA_HEADS * (QK_NOPE + V_DIM)), KV_LORA),
        "mla_w_out": _w(ks[29], (N_MLA, MLA_HEADS * V_DIM, D_MODEL), MLA_HEADS * V_DIM),
        "final_norm": _gain(ks[30], (D_MODEL,)),
    }


def reference(x_prompt, x_sample, mem_prompt, mem_sample,
              ffn1_norm, ffn1_w_gate, ffn1_w_up, ffn1_w_down,
              mix_norm, xattn_norm, mem_norm, xattn_w_q, xattn_w_kv, xattn_w_o,
              ffn2_norm, ffn2_w_gate, ffn2_w_up, ffn2_w_down,
              ab_w_in, ab_conv_w, ab_conv_b, ab_conv_ln_g, ab_conv_ln_b, ab_w_out,
              mla_w_in, mla_q_norm, mla_w_q_b, mla_kv_norm, mla_w_kv_b, mla_w_out,
              final_norm):
    p = {
        "ffn1_norm": ffn1_norm, "ffn1_w_gate": ffn1_w_gate, "ffn1_w_up": ffn1_w_up, "ffn1_w_down": ffn1_w_down,
        "mix_norm": mix_norm, "xattn_norm": xattn_norm, "mem_norm": mem_norm,
        "xattn_w_q": xattn_w_q, "xattn_w_kv": xattn_w_kv, "xattn_w_o": xattn_w_o,
        "ffn2_norm": ffn2_norm, "ffn2_w_gate": ffn2_w_gate, "ffn2_w_up": ffn2_w_up, "ffn2_w_down": ffn2_w_down,
        "ab_w_in": ab_w_in, "ab_conv_w": ab_conv_w, "ab_conv_b": ab_conv_b,
        "ab_conv_ln_g": ab_conv_ln_g, "ab_conv_ln_b": ab_conv_ln_b, "ab_w_out": ab_w_out,
        "mla_w_in": mla_w_in, "mla_q_norm": mla_q_norm, "mla_w_q_b": mla_w_q_b,
        "mla_kv_norm": mla_kv_norm, "mla_w_kv_b": mla_w_kv_b, "mla_w_out": mla_w_out,
        "final_norm": final_norm,
    }
    y_prompt = trunk(x_prompt, mem_prompt, p)
    y_sample = trunk(x_sample, mem_sample, p)
    return (y_prompt, y_sample)
```

```python
import functools
import math

import numpy as np
import jax
import jax.numpy as jnp
from jax import lax
from jax.experimental import pallas as pl
from jax.experimental.pallas import tpu as pltpu

D_MODEL = 1024
SEQ = 2048
DEPTH = 4
MEM_LEN = 256
D_FF = 2816
C_A = 512
A_GROUPS = 4
A_GROUP_DIM = 128
C_B = 512
CONV_WIDTH = 31
CONV_PAD = 15
MLA_HEADS = 8
QK_NOPE = 128
QK_ROPE = 64
V_DIM = 128
Q_LORA = 512
KV_LORA = 256
ROPE_BASE = 10000.0
XA_HEADS = 4
XA_HEAD_DIM = 256
EPS = 1e-6

F32 = jnp.float32
BF16 = jnp.bfloat16

VMEM_LIMIT_BYTES = 56 * 1024 * 1024

TOKEN_TILE = 512
FFN_CHUNK = 256
MLA_Q_TILE = 512
MLA_QK = 256
CONV_ROWS = 64
CONV_HALO = 16
DFT_ROWS = 512
LANES = 128
SUBLANES = 8
MEM_TILE = 1024


def _params(*sem):
    return pltpu.CompilerParams(dimension_semantics=sem, vmem_limit_bytes=VMEM_LIMIT_BYTES)


def _const_spec(shape):
    zeros = (0,) * len(shape)
    return pl.BlockSpec(shape, lambda *_: zeros, pipeline_mode=pl.Buffered(1))


def _rms(xf, g):
    ms = jnp.mean(xf * xf, axis=-1, keepdims=True)
    return xf * lax.rsqrt(ms + EPS) * g


def _dot(a, b):
    return jnp.dot(a, b, preferred_element_type=F32)


def _ffn_kernel(x_ref, g_ref, wg_ref, wu_ref, wd_ref, *rest, final):
    if final:
        gf_ref, o_ref, z_ref = rest
    else:
        o_ref, z_ref = rest
    x = x_ref[...]
    h = _rms(x, g_ref[...]).astype(BF16)
    for c in range(D_FF // FFN_CHUNK):
        sl = slice(c * FFN_CHUNK, (c + 1) * FFN_CHUNK)
        a = _dot(h, wg_ref[:, sl])
        b = _dot(h, wu_ref[:, sl])
        z_ref[:, sl] = (a * jax.nn.sigmoid(a) * b).astype(BF16)
    y = x + 0.5 * _dot(z_ref[...], wd_ref[...])
    if final:
        y = _rms(y, gf_ref[...])
    o_ref[...] = y


def _ffn(x, g, wg, wu, wd, final_g=None):
    t = x.shape[0]
    final = final_g is not None
    tok = pl.BlockSpec((TOKEN_TILE, D_MODEL), lambda i: (i, 0))
    in_specs = [tok, _const_spec((1, D_MODEL)), _const_spec((D_MODEL, D_FF)),
                _const_spec((D_MODEL, D_FF)), _const_spec((D_FF, D_MODEL))]
    args = [x, g, wg, wu, wd]
    if final:
        in_specs.append(_const_spec((1, D_MODEL)))
        args.append(final_g)
    return pl.pallas_call(
        functools.partial(_ffn_kernel, final=final),
        grid=(t // TOKEN_TILE,),
        in_specs=in_specs,
        out_specs=tok,
        out_shape=jax.ShapeDtypeStruct((t, D_MODEL), F32),
        scratch_shapes=[pltpu.VMEM((TOKEN_TILE, D_FF), BF16)],
        compiler_params=_params("parallel"),
        name="ffn_final" if final else "ffn",
    )(*args)


def _xkv_kernel(m_ref, g_ref, w_ref, kv_ref):
    hn = _rms(m_ref[...], g_ref[0]).astype(BF16)
    kv_ref[0] = _dot(hn, w_ref[0]).astype(BF16)


def _xkv(mem, g, w):
    tm = mem.shape[0]
    return pl.pallas_call(
        _xkv_kernel,
        grid=(DEPTH, tm // MEM_TILE),
        in_specs=[pl.BlockSpec((MEM_TILE, D_MODEL), lambda l, i: (i, 0)),
                  pl.BlockSpec((1, 1, D_MODEL), lambda l, i: (l, 0, 0)),
                  pl.BlockSpec((1, D_MODEL, 2 * D_MODEL), lambda l, i: (l, 0, 0))],
        out_specs=pl.BlockSpec((1, MEM_TILE, 2 * D_MODEL), lambda l, i: (l, i, 0)),
        out_shape=jax.ShapeDtypeStruct((DEPTH, tm, 2 * D_MODEL), BF16),
        compiler_params=_params("parallel", "parallel"),
        name="xattn_kv",
    )(mem, g, w)


def _xattn_kernel(x_ref, y_ref, wout_ref, g_ref, wq_ref, kv_ref, wo_ref, o_ref, *, y_transposed):
    if y_transposed:
        mix = lax.dot_general(y_ref[0], wout_ref[...], (((0,), (0,)), ((), ())),
                              preferred_element_type=F32)
    else:
        mix = _dot(y_ref[...], wout_ref[...])
    x1 = x_ref[...] + mix
    h = _rms(x1, g_ref[...]).astype(BF16)
    q = (_dot(h, wq_ref[...]) * (1.0 / math.sqrt(XA_HEAD_DIM))).astype(BF16)
    heads = []
    for hd in range(XA_HEADS):
        lo = hd * XA_HEAD_DIM
        kh = kv_ref[:, lo:lo + XA_HEAD_DIM]
        vh = kv_ref[:, D_MODEL + lo:D_MODEL + lo + XA_HEAD_DIM]
        s = lax.dot_general(q[:, lo:lo + XA_HEAD_DIM], kh, (((1,), (1,)), ((), ())),
                            preferred_element_type=F32)
        p = jnp.exp(s - jnp.max(s, axis=-1, keepdims=True))
        inv = 1.0 / jnp.sum(p, axis=-1, keepdims=True)
        heads.append((_dot(p.astype(BF16), vh) * inv).astype(BF16))
    o = jnp.concatenate(heads, axis=-1)
    o_ref[...] = x1 + _dot(o, wo_ref[...])


def _xattn(x, y, w_out, g, wq, kv, wo, *, y_transposed):
    t = x.shape[0]
    tiles_per_seq = SEQ // TOKEN_TILE
    tok = pl.BlockSpec((TOKEN_TILE, D_MODEL), lambda i: (i, 0))
    if y_transposed:
        y_spec = pl.BlockSpec((1, D_MODEL, TOKEN_TILE),
                              lambda i: (i // tiles_per_seq, 0, i % tiles_per_seq))
    else:
        y_spec = tok
    return pl.pallas_call(
        functools.partial(_xattn_kernel, y_transposed=y_transposed),
        grid=(t // TOKEN_TILE,),
        in_specs=[tok, y_spec, _const_spec((D_MODEL, D_MODEL)), _const_spec((1, D_MODEL)),
                  _const_spec((D_MODEL, D_MODEL)),
                  pl.BlockSpec((MEM_LEN, 2 * D_MODEL), lambda i: (i // tiles_per_seq, 0)),
                  _const_spec((D_MODEL, D_MODEL))],
        out_specs=tok,
        out_shape=jax.ShapeDtypeStruct((t, D_MODEL), F32),
        compiler_params=_params("parallel"),
        name="xattn_t" if y_transposed else "xattn",
    )(x, y, w_out, g, wq, kv, wo)


def _ab_in_kernel(x_ref, g_ref, win_ref, dft_ref, a_ref, b_ref, hb_ref):
    h = _rms(x_ref[...], g_ref[...]).astype(BF16)
    u = _dot(h, win_ref[...])
    for gi in range(A_GROUPS):
        lo = gi * A_GROUP_DIM
        ab = _dot(u[:, lo:lo + A_GROUP_DIM].astype(BF16), dft_ref[...])
        a_ref[:, lo:lo + A_GROUP_DIM] = ab[:, :A_GROUP_DIM].astype(BF16)
        b_ref[:, lo:lo + A_GROUP_DIM] = ab[:, A_GROUP_DIM:].astype(BF16)
    hb_ref[...] = u[:, C_A:C_A + C_B] * jax.nn.sigmoid(u[:, C_A + C_B:])


def _ab_in(x, g, w_in, dft_c):
    t = x.shape[0]
    tok = pl.BlockSpec((TOKEN_TILE, D_MODEL), lambda i: (i, 0))
    half = pl.BlockSpec((TOKEN_TILE, C_A), lambda i: (i, 0))
    return pl.pallas_call(
        _ab_in_kernel,
        grid=(t // TOKEN_TILE,),
        in_specs=[tok, _const_spec((1, D_MODEL)), _const_spec((D_MODEL, C_A + 2 * C_B)),
                  _const_spec((A_GROUP_DIM, 2 * A_GROUP_DIM))],
        out_specs=[half, half, half],
        out_shape=[jax.ShapeDtypeStruct((t, C_A), BF16), jax.ShapeDtypeStruct((t, C_A), BF16),
                   jax.ShapeDtypeStruct((t, C_B), F32)],
        compiler_params=_params("parallel"),
        name="ab_in",
    )(x, g, w_in, dft_c)


def _fconv_kernel(a_ref, b_ref, hb_ref, cs_ref, ss_ref, cw_ref, cb_ref, lg_ref, lb_ref,
                  y_ref, pad_ref, conv_ref):
    for r in range(SEQ // DFT_ROWS):
        rs = slice(r * DFT_ROWS, (r + 1) * DFT_ROWS)
        ya = _dot(cs_ref[rs, :], a_ref[...]) - _dot(ss_ref[rs, :], b_ref[...])
        y_ref[rs, :C_A] = ya.astype(BF16)

    zeros = jnp.zeros((CONV_HALO, C_B), F32)
    pad_ref[:CONV_HALO, :] = zeros
    pad_ref[CONV_HALO + SEQ:, :] = zeros
    pad_ref[CONV_HALO:CONV_HALO + SEQ, :] = hb_ref[...]

    win_rows = CONV_ROWS + 2 * CONV_HALO

    def rows(i, carry):
        r0 = pl.multiple_of(i * CONV_ROWS, CONV_ROWS)
        for c in range(C_B // LANES):
            ls = slice(c * LANES, (c + 1) * LANES)
            win = pad_ref[pl.ds(r0, win_rows), ls]
            acc = jnp.broadcast_to(cb_ref[:, ls], (CONV_ROWS, LANES))
            for s in range(SUBLANES):
                shifted = win if s == 0 else pltpu.roll(win, win_rows - s, 0)
                for q in range(2 * CONV_HALO // SUBLANES):
                    k = q * SUBLANES + s - 1
                    if 0 <= k < CONV_WIDTH:
                        acc = acc + cw_ref[k:k + 1, ls] * shifted[q * SUBLANES:q * SUBLANES + CONV_ROWS]
            conv_ref[:, ls] = acc
        acc = conv_ref[...]
        mu = jnp.mean(acc, axis=-1, keepdims=True)
        xc = acc - mu
        var = jnp.mean(xc * xc, axis=-1, keepdims=True)
        ln = xc * lax.rsqrt(var + EPS) * lg_ref[...] + lb_ref[...]
        y_ref[pl.ds(r0, CONV_ROWS), C_A:] = (ln * jax.nn.sigmoid(ln)).astype(BF16)
        return carry

    lax.fori_loop(0, SEQ // CONV_ROWS, rows, 0)


def _fconv(a, b, hb, cs, ss, cw, cb, lg, lb):
    t = a.shape[0]
    half = pl.BlockSpec((SEQ, C_A), lambda i: (i, 0))
    return pl.pallas_call(
        _fconv_kernel,
        grid=(t // SEQ,),
        in_specs=[half, half, half, _const_spec((SEQ, SEQ)), _const_spec((SEQ, SEQ)),
                  _const_spec((CONV_WIDTH, C_B)), _const_spec((1, C_B)), _const_spec((1, C_B)),
                  _const_spec((1, C_B))],
        out_specs=pl.BlockSpec((SEQ, D_MODEL), lambda i: (i, 0)),
        out_shape=jax.ShapeDtypeStruct((t, D_MODEL), BF16),
        scratch_shapes=[pltpu.VMEM((SEQ + 2 * CONV_HALO, C_B), F32),
                        pltpu.VMEM((CONV_ROWS, C_B), F32)],
        compiler_params=_params("parallel"),
        name="fourier_conv",
    )(a, b, hb, cs, ss, cw, cb, lg, lb)


def _mla_in_kernel(x_ref, g_ref, win_ref, qn_ref, wq_ref, kvn_ref, wkv_ref, tq_ref, tk_ref,
                   q_ref, k_ref, v_ref):
    h = _rms(x_ref[...], g_ref[...]).astype(BF16)
    u = _dot(h, win_ref[...])
    cqn = _rms(u[:, :Q_LORA], qn_ref[...]).astype(BF16)
    q = _dot(cqn, wq_ref[...])
    tq = tq_ref[...]
    for hd in range(MLA_HEADS):
        sl = slice(hd * MLA_QK, (hd + 1) * MLA_QK)
        q_ref[:, sl] = (q[:, sl] * tq).astype(BF16)
    ckvn = _rms(u[:, Q_LORA:Q_LORA + KV_LORA], kvn_ref[...]).astype(BF16)
    kv = _dot(ckvn, wkv_ref[...])
    kr = u[:, Q_LORA + KV_LORA:] * tk_ref[...]
    krr = (kr + pltpu.roll(kr, QK_ROPE, 1)).astype(BF16)
    for hd in range(MLA_HEADS):
        k_ref[:, hd * MLA_QK:hd * MLA_QK + QK_NOPE] = kv[:, hd * QK_NOPE:(hd + 1) * QK_NOPE].astype(BF16)
        k_ref[:, hd * MLA_QK + QK_NOPE:(hd + 1) * MLA_QK] = krr
    v_ref[...] = kv[:, MLA_HEADS * QK_NOPE:].astype(BF16)


def _mla_in(x, g, w_in, qn, wq, kvn, wkv, tq, tk):
    t = x.shape[0]
    tiles_per_seq = SEQ // TOKEN_TILE
    tok = pl.BlockSpec((TOKEN_TILE, D_MODEL), lambda i: (i, 0))
    wide = pl.BlockSpec((TOKEN_TILE, MLA_HEADS * MLA_QK), lambda i: (i, 0))
    w_in_cols = Q_LORA + KV_LORA + 2 * QK_ROPE
    return pl.pallas_call(
        _mla_in_kernel,
        grid=(t // TOKEN_TILE,),
        in_specs=[tok, _const_spec((1, D_MODEL)), _const_spec((D_MODEL, w_in_cols)),
                  _const_spec((1, Q_LORA)), _const_spec((Q_LORA, MLA_HEADS * MLA_QK)),
                  _const_spec((1, KV_LORA)), _const_spec((KV_LORA, MLA_HEADS * (QK_NOPE + V_DIM))),
                  pl.BlockSpec((TOKEN_TILE, MLA_QK), lambda i: (i % tiles_per_seq, 0)),
                  pl.BlockSpec((TOKEN_TILE, 2 * QK_ROPE), lambda i: (i % tiles_per_seq, 0))],
        out_specs=[wide, wide, tok],
        out_shape=[jax.ShapeDtypeStruct((t, MLA_HEADS * MLA_QK), BF16),
                   jax.ShapeDtypeStruct((t, MLA_HEADS * MLA_QK), BF16),
                   jax.ShapeDtypeStruct((t, MLA_HEADS * V_DIM), BF16)],
        compiler_params=_params("parallel"),
        name="mla_in",
    )(x, g, w_in, qn, wq, kvn, wkv, tq, tk)


def _mla_attn_kernel(q_ref, k_ref, v_ref, o_ref):
    st = lax.dot_general(k_ref[...], q_ref[...], (((1,), (1,)), ((), ())),
                         preferred_element_type=F32)
    p = jnp.exp(st - jnp.max(st, axis=0, keepdims=True))
    inv = 1.0 / jnp.sum(p, axis=0, keepdims=True)
    ot = lax.dot_general(v_ref[...], p.astype(BF16), (((0,), (0,)), ((), ())),
                         preferred_element_type=F32)
    o_ref[0] = (ot * inv).astype(BF16)


def _mla_attn(q, k, v):
    nseq = q.shape[0] // SEQ
    q_tiles = SEQ // MLA_Q_TILE
    return pl.pallas_call(
        _mla_attn_kernel,
        grid=(nseq, MLA_HEADS, q_tiles),
        in_specs=[pl.BlockSpec((MLA_Q_TILE, MLA_QK), lambda b, h, i: (b * q_tiles + i, h)),
                  pl.BlockSpec((SEQ, MLA_QK), lambda b, h, i: (b, h)),
                  pl.BlockSpec((SEQ, V_DIM), lambda b, h, i: (b, h))],
        out_specs=pl.BlockSpec((1, V_DIM, MLA_Q_TILE), lambda b, h, i: (b, h, i)),
        out_shape=jax.ShapeDtypeStruct((nseq, MLA_HEADS * V_DIM, SEQ), BF16),
        compiler_params=_params("parallel", "parallel", "arbitrary"),
        name="mla_attn",
    )(q, k, v)


def _dft_tables():
    def cos_sin(n):
        kn = np.outer(np.arange(n), np.arange(n)) % n
        ang = 2.0 * np.pi * kn.astype(np.float64) / n
        return np.cos(ang) / math.sqrt(n), np.sin(ang) / math.sqrt(n)

    cs, ss = cos_sin(SEQ)
    cc, sc = cos_sin(A_GROUP_DIM)
    dft_c = np.concatenate([cc, sc], axis=1)
    return (jnp.asarray(cs, BF16), jnp.asarray(ss, BF16), jnp.asarray(dft_c, BF16))


def _rope_tables():
    half = QK_ROPE // 2
    inv_freq = 1.0 / (ROPE_BASE ** (jnp.arange(0, QK_ROPE, 2, dtype=F32) / QK_ROPE))
    ang = jnp.arange(SEQ, dtype=F32)[:, None] * inv_freq[None, :]
    cos, sin = jnp.cos(ang), jnp.sin(ang)
    cos2 = jnp.concatenate([cos, cos], axis=-1)
    sin2 = jnp.concatenate([-sin, sin], axis=-1)
    scale = 1.0 / math.sqrt(QK_NOPE + QK_ROPE)
    tq = jnp.concatenate([jnp.ones((SEQ, QK_NOPE), F32), cos2, sin2], axis=-1) * scale
    tk = jnp.concatenate([cos2, sin2], axis=-1)
    del half
    return tq, tk


def _mla_weights(w_in, w_q_b, w_kv_b):
    half = QK_ROPE // 2
    r0 = Q_LORA + KV_LORA
    w_in2 = jnp.concatenate([w_in, w_in[:, r0 + half:], w_in[:, r0:r0 + half]], axis=1)
    hq = QK_NOPE + QK_ROPE
    q_cols = []
    for hd in range(MLA_HEADS):
        base = hd * hq
        q_cols += [w_q_b[:, base:base + hq],
                   w_q_b[:, base + QK_NOPE + half:base + hq],
                   w_q_b[:, base + QK_NOPE:base + QK_NOPE + half]]
    wq2 = jnp.concatenate(q_cols, axis=1)
    hkv = QK_NOPE + V_DIM
    k_cols = [w_kv_b[:, hd * hkv:hd * hkv + QK_NOPE] for hd in range(MLA_HEADS)]
    v_cols = [w_kv_b[:, hd * hkv + QK_NOPE:(hd + 1) * hkv] for hd in range(MLA_HEADS)]
    wkv2 = jnp.concatenate(k_cols + v_cols, axis=1)
    return w_in2.astype(BF16), wq2.astype(BF16), wkv2.astype(BF16)


def kernel(x_prompt, x_sample, mem_prompt, mem_sample, ffn1_norm, ffn1_w_gate, ffn1_w_up, ffn1_w_down, mix_norm, xattn_norm, mem_norm, xattn_w_q, xattn_w_kv, xattn_w_o, ffn2_norm, ffn2_w_gate, ffn2_w_up, ffn2_w_down, ab_w_in, ab_conv_w, ab_conv_b, ab_conv_ln_g, ab_conv_ln_b, ab_w_out, mla_w_in, mla_q_norm, mla_w_q_b, mla_kv_norm, mla_w_kv_b, mla_w_out, final_norm):
    nb_p, nb_s = x_prompt.shape[0], x_sample.shape[0]
    x = jnp.concatenate([x_prompt.reshape(-1, D_MODEL), x_sample.reshape(-1, D_MODEL)], axis=0)
    mem = jnp.concatenate([mem_prompt.reshape(-1, D_MODEL), mem_sample.reshape(-1, D_MODEL)], axis=0)

    bf = lambda w: w.astype(BF16)
    row = lambda v: v.reshape(1, -1)
    cs, ss, dft_c = _dft_tables()
    tq, tk = _rope_tables()

    kv_all = _xkv(mem, mem_norm.reshape(DEPTH, 1, D_MODEL), bf(xattn_w_kv))

    for l in range(DEPTH):
        i = l // 2
        x = _ffn(x, row(ffn1_norm[l]), bf(ffn1_w_gate[l]), bf(ffn1_w_up[l]), bf(ffn1_w_down[l]))
        if l % 2 == 0:
            a, b, hb = _ab_in(x, row(mix_norm[l]), bf(ab_w_in[i]), dft_c)
            y = _fconv(a, b, hb, cs, ss, ab_conv_w[i], row(ab_conv_b[i]),
                       row(ab_conv_ln_g[i]), row(ab_conv_ln_b[i]))
            w_out = bf(ab_w_out[i])
        else:
            w_in2, wq2, wkv2 = _mla_weights(mla_w_in[i], mla_w_q_b[i], mla_w_kv_b[i])
            q, k, v = _mla_in(x, row(mix_norm[l]), w_in2, row(mla_q_norm[i]), wq2,
                              row(mla_kv_norm[i]), wkv2, tq, tk)
            y = _mla_attn(q, k, v)
            w_out = bf(mla_w_out[i])
        x = _xattn(x, y, w_out, row(xattn_norm[l]), bf(xattn_w_q[l]), kv_all[l],
                   bf(xattn_w_o[l]), y_transposed=(l % 2 == 1))
        x = _ffn(x, row(ffn2_norm[l]), bf(ffn2_w_gate[l]), bf(ffn2_w_up[l]), bf(ffn2_w_down[l]),
                 final_g=row(final_norm) if l == DEPTH - 1 else None)

    t_p = nb_p * SEQ
    y_prompt = x[:t_p].reshape(nb_p, SEQ, D_MODEL)
    y_sample = x[t_p:].reshape(nb_s, SEQ, D_MODEL)
    return (y_prompt, y_sample)
```
